```python
import math
import jax, jax.numpy as jnp
from jax import lax
import numpy as np

D_MODEL = 2048
BATCH = 2
SEQ = 4096
DEPTH = 1
DEC_BATCH = 32
DEC_SEQ = 1
PAST_LEN = 8192
PAGE_SIZE = 128

HEAD_DIM = 128
N_ATTN_HEADS = 8
ATTN_W = N_ATTN_HEADS * HEAD_DIM
CONV_W = D_MODEL // 4
N_MEM_HEADS = 4
MEM_W = N_MEM_HEADS * HEAD_DIM
MIX_W = ATTN_W + CONV_W + MEM_W
IN_W = 3 * ATTN_W + 2 * CONV_W + MEM_W
SPLITS = (ATTN_W, 2 * ATTN_W, 3 * ATTN_W, 3 * ATTN_W + CONV_W, 3 * ATTN_W + 2 * CONV_W)
MOBA_BLOCK = 256
MOBA_TOPK = 3
CONV_K = 31
FFN_K = 3
D_FF = 5632
N_MEM = 256
Q_CHUNK = 64
NORM_EPS = 1e-6
LN_EPS = 1e-5

kernel_name = 'hymba_moba_conformer_convglu_step'


def rmsnorm(x, g):
    xf = x.astype(jnp.float32)
    y = xf * lax.rsqrt(jnp.mean(xf * xf, axis=-1, keepdims=True) + NORM_EPS)
    return (y * g.astype(jnp.float32)).astype(x.dtype)


def layernorm(x, g, b):
    xf = x.astype(jnp.float32)
    xc = xf - jnp.mean(xf, axis=-1, keepdims=True)
    y = xc * lax.rsqrt(jnp.mean(xc * xc, axis=-1, keepdims=True) + LN_EPS)
    return (y * g.astype(jnp.float32) + b.astype(jnp.float32)).astype(x.dtype)


def causal_dwconv(u_hist, w, b):
    c = u_hist.shape[-1]
    y = lax.conv_general_dilated(u_hist, w[:, None, :].astype(u_hist.dtype), window_strides=(1,),
                                 padding='VALID', dimension_numbers=('NWC', 'WIO', 'NWC'),
                                 feature_group_count=c)
    return y + b.astype(u_hist.dtype)


def moba_attend(q, q_pos, kb, vb, k_mean):
    b, nb, blk, h, d = kb.shape
    nq = q.shape[1]
    topk = min(MOBA_TOPK, nb)
    own = q_pos // MOBA_BLOCK
    gate = jnp.einsum('bqhd,bnhd->bhqn', q.astype(jnp.float32), k_mean)
    fully_past = jnp.arange(nb)[None, :] < own[:, None]
    gate = jnp.where(fully_past, gate, -jnp.inf)
    _, sel = lax.top_k(gate, topk)
    sel_ok = sel < own[:, None]
    blocks = jnp.concatenate([sel, jnp.broadcast_to(own[:, None], (b, h, nq, 1))], axis=-1)
    blk_ok = jnp.concatenate([sel_ok, jnp.ones((b, h, nq, 1), bool)], axis=-1)
    bi = jnp.arange(b)[:, None, None, None]
    hi = jnp.arange(h)[None, :, None, None]
    kg = kb[bi, blocks, :, hi]
    vg = vb[bi, blocks, :, hi]
    s = jnp.einsum('bqhd,bhqnkd->bhqnk', q, kg).astype(jnp.float32) * (HEAD_DIM ** -0.5)
    key_pos = blocks[..., None] * MOBA_BLOCK + jnp.arange(blk)
    mask = blk_ok[..., None] & (key_pos <= q_pos[:, None, None])
    s = jnp.where(mask, s, -jnp.inf)
    p = jax.nn.softmax(s.reshape(b, h, nq, -1), axis=-1).reshape(s.shape)
    return jnp.einsum('bhqnk,bhqnkd->bqhd', p.astype(vg.dtype), vg)


def to_blocks(k, v):
    b, l_pad, h, d = k.shape
    kb = k.reshape(b, l_pad // MOBA_BLOCK, MOBA_BLOCK, h, d)
    vb = v.reshape(b, l_pad // MOBA_BLOCK, MOBA_BLOCK, h, d)
    return kb, vb, jnp.mean(kb.astype(jnp.float32), axis=2)


def moba_prompt(q, k, v):
    b, s, h, d = q.shape
    s_pad = -(-s // MOBA_BLOCK) * MOBA_BLOCK
    pad = ((0, 0), (0, s_pad - s), (0, 0), (0, 0))
    kb, vb, k_mean = to_blocks(jnp.pad(k, pad), jnp.pad(v, pad))
    n_chunks = s // Q_CHUNK
    qc = q.reshape(b, n_chunks, Q_CHUNK, h, d).transpose(1, 0, 2, 3, 4)
    pos = jnp.arange(s, dtype=jnp.int32).reshape(n_chunks, Q_CHUNK)
    out = lax.map(lambda a: moba_attend(a[0], a[1], kb, vb, k_mean), (qc, pos))
    return out.transpose(1, 0, 2, 3, 4).reshape(b, s, h, d)


def mem_attend(q, mk, mv):
    s = jnp.einsum('bthd,bmhd->bhtm', q, mk).astype(jnp.float32) * (HEAD_DIM ** -0.5)
    p = jax.nn.softmax(s, axis=-1)
    return jnp.einsum('bhtm,bmhd->bthd', p.astype(mv.dtype), mv)


def hybrid_layer(x, attend, conv_hist, ffn_hist, mem_k, mem_v,
                 g_mix, w_in, w_cdw, b_cdw, g_cln, b_cln, w_out,
                 g_ffn, w_gate, w_val, w_fdw, b_fdw, w_down):
    b, t, _ = x.shape
    h = rmsnorm(x, g_mix)
    q, k, v, ca, cg, qm = jnp.split(h @ w_in, SPLITS, axis=-1)
    q = q.reshape(b, t, N_ATTN_HEADS, HEAD_DIM)
    k = k.reshape(b, t, N_ATTN_HEADS, HEAD_DIM)
    v = v.reshape(b, t, N_ATTN_HEADS, HEAD_DIM)
    o_attn = attend(q, k, v).reshape(b, t, ATTN_W)
    u_hist = jnp.concatenate([conv_hist, ca * jax.nn.sigmoid(cg)], axis=1)
    o_conv = jax.nn.silu(layernorm(causal_dwconv(u_hist, w_cdw, b_cdw), g_cln, b_cln))
    o_mem = mem_attend(qm.reshape(b, t, N_MEM_HEADS, HEAD_DIM), mem_k, mem_v).reshape(b, t, MEM_W)
    x = x + jnp.concatenate([o_attn, o_conv, o_mem], axis=-1) @ w_out
    h2 = rmsnorm(x, g_ffn)
    a_hist = jnp.concatenate([ffn_hist, h2 @ w_gate], axis=1)
    gate = causal_dwconv(a_hist, w_fdw, b_fdw)
    x = x + (jax.nn.silu(gate) * (h2 @ w_val)) @ w_down
    return x, k, v, u_hist[:, -(CONV_K - 1):], a_hist[:, -(FFN_K - 1):]


def setup_inputs(seed: int = 0) -> dict:
    key = jax.random.key(seed)
    ks = iter(jax.random.split(key, 40))
    nrm = lambda shape, scale: jax.random.normal(next(ks), shape, jnp.float32) * scale
    gain = lambda shape: 1.0 + nrm(shape, 0.02)
    n_pages = PAST_LEN // PAGE_SIZE
    n_used = DEC_BATCH * n_pages
    n_pool = n_used + max(1, n_used // 4)
    perm = jax.random.permutation(next(ks), n_pool)
    page_table = perm[:n_used].reshape(DEC_BATCH, n_pages).astype(jnp.int32)
    return {
        'x_prompt': nrm((BATCH, SEQ, D_MODEL), 1.0),
        'x_sample': nrm((DEC_BATCH, DEC_SEQ, D_MODEL), 1.0),
        'cache_k': nrm((DEPTH, n_pool, PAGE_SIZE, N_ATTN_HEADS, HEAD_DIM), 1.0),
        'cache_v': nrm((DEPTH, n_pool, PAGE_SIZE, N_ATTN_HEADS, HEAD_DIM), 1.0),
        'state_conv': nrm((DEPTH, DEC_BATCH, CONV_K - 1, CONV_W), 0.5),
        'state_ffn': nrm((DEPTH, DEC_BATCH, FFN_K - 1, D_FF), 1.0),
        'cache_mem_k': nrm((DEPTH, DEC_BATCH, N_MEM, N_MEM_HEADS, HEAD_DIM), 1.0),
        'cache_mem_v': nrm((DEPTH, DEC_BATCH, N_MEM, N_MEM_HEADS, HEAD_DIM), 1.0),
        'page_table': page_table,
        'mem_prompt': nrm((BATCH, N_MEM, D_MODEL), 1.0),
        'g_mix': gain((DEPTH, D_MODEL)),
        'w_in': nrm((DEPTH, D_MODEL, IN_W), D_MODEL ** -0.5),
        'w_conv_dw': nrm((DEPTH, CONV_K, CONV_W), CONV_K ** -0.5),
        'b_conv_dw': nrm((DEPTH, CONV_W), 0.02),
        'g_conv_ln': gain((DEPTH, CONV_W)),
        'b_conv_ln': nrm((DEPTH, CONV_W), 0.02),
        'w_out': nrm((DEPTH, MIX_W, D_MODEL), MIX_W ** -0.5),
        'g_mem': gain((DEPTH, D_MODEL)),
        'w_mem_k': nrm((DEPTH, D_MODEL, MEM_W), D_MODEL ** -0.5),
        'w_mem_v': nrm((DEPTH, D_MODEL, MEM_W), D_MODEL ** -0.5),
        'g_ffn': gain((DEPTH, D_MODEL)),
        'w_ffn_gate': nrm((DEPTH, D_MODEL, D_FF), D_MODEL ** -0.5),
        'w_ffn_val': nrm((DEPTH, D_MODEL, D_FF), D_MODEL ** -0.5),
        'w_ffn_dw': nrm((DEPTH, FFN_K, D_FF), FFN_K ** -0.5),
        'b_ffn_dw': nrm((DEPTH, D_FF), 0.02),
        'w_ffn_down': nrm((DEPTH, D_FF, D_MODEL), D_FF ** -0.5),
        'g_final': gain((D_MODEL,)),
    }


def reference(x_prompt, x_sample, cache_k, cache_v, state_conv, state_ffn, cache_mem_k, cache_mem_v,
              page_table, mem_prompt, g_mix, w_in, w_conv_dw, b_conv_dw, g_conv_ln, b_conv_ln, w_out,
              g_mem, w_mem_k, w_mem_v, g_ffn, w_ffn_gate, w_ffn_val, w_ffn_dw, b_ffn_dw, w_ffn_down,
              g_final):
    bp = x_prompt.shape[0]
    db, dsq, _ = x_sample.shape
    past = page_table.shape[1] * cache_k.shape[2]
    l_all = past + dsq
    l_pad = -(-l_all // MOBA_BLOCK) * MOBA_BLOCK
    pos_s = past + jnp.arange(dsq, dtype=jnp.int32)
    zero_conv = jnp.zeros((bp, CONV_K - 1, CONV_W), x_prompt.dtype)
    zero_ffn = jnp.zeros((bp, FFN_K - 1, D_FF), x_prompt.dtype)
    xp, xs = x_prompt, x_sample
    kp_l, vp_l, ks_l, vs_l, cp_l, cs_l, fp_l, fs_l, mkp_l, mvp_l = ([] for _ in range(10))
    for l in range(DEPTH):
        wl = (g_mix[l], w_in[l], w_conv_dw[l], b_conv_dw[l], g_conv_ln[l], b_conv_ln[l], w_out[l],
              g_ffn[l], w_ffn_gate[l], w_ffn_val[l], w_ffn_dw[l], b_ffn_dw[l], w_ffn_down[l])
        hm = rmsnorm(mem_prompt, g_mem[l])
        mk_p = (hm @ w_mem_k[l]).reshape(bp, N_MEM, N_MEM_HEADS, HEAD_DIM)
        mv_p = (hm @ w_mem_v[l]).reshape(bp, N_MEM, N_MEM_HEADS, HEAD_DIM)
        xp, kp, vp, cp, fp = hybrid_layer(xp, moba_prompt, zero_conv, zero_ffn, mk_p, mv_p, *wl)

        def attend_sample(q, k, v, l=l):
            hh, dd = k.shape[2], k.shape[3]
            zpad = jnp.zeros((db, l_pad - l_all, hh, dd), k.dtype)
            k_past = cache_k[l, page_table].reshape(db, past, hh, dd)
            v_past = cache_v[l, page_table].reshape(db, past, hh, dd)
            kb, vb, k_mean = to_blocks(jnp.concatenate([k_past.astype(k.dtype), k, zpad], axis=1),
                                       jnp.concatenate([v_past.astype(v.dtype), v, zpad], axis=1))
            return moba_attend(q, pos_s, kb, vb, k_mean)

        xs, ks, vs, cs, fs = hybrid_layer(xs, attend_sample, state_conv[l], state_ffn[l],
                                          cache_mem_k[l], cache_mem_v[l], *wl)
        kp_l.append(kp); vp_l.append(vp); ks_l.append(ks); vs_l.append(vs)
        cp_l.append(cp); cs_l.append(cs); fp_l.append(fp); fs_l.append(fs)
        mkp_l.append(mk_p); mvp_l.append(mv_p)
    y_prompt = rmsnorm(xp, g_final)
    y_sample = rmsnorm(xs, g_final)
    return (y_prompt, y_sample,
            jnp.stack(kp_l), jnp.stack(vp_l), jnp.stack(ks_l), jnp.stack(vs_l),
            jnp.stack(cp_l), jnp.stack(cs_l), jnp.stack(fp_l), jnp.stack(fs_l),
            jnp.stack(mkp_l), jnp.stack(mvp_l))
```

```python
import functools
import math

import jax
import jax.numpy as jnp
from jax import lax
from jax.experimental import pallas as pl
from jax.experimental.pallas import tpu as pltpu

F32 = jnp.float32
BF16 = jnp.bfloat16

HEAD_DIM = 128
N_ATTN_HEADS = 8
ATTN_W = N_ATTN_HEADS * HEAD_DIM
N_MEM_HEADS = 4
MEM_W = N_MEM_HEADS * HEAD_DIM
CONV_W = 512
MOBA_BLOCK = 256
MOBA_TOPK = 3
CONV_K = 31
FFN_K = 3
NORM_EPS = 1e-6
LN_EPS = 1e-5
SCALE = HEAD_DIM ** -0.5
EXP2_SCALE = SCALE * math.log2(math.e)
NEG_INF = float("-inf")

K_COL = ATTN_W // HEAD_DIM
V_COL = 2 * ATTN_W // HEAD_DIM
CA_COL = 3 * ATTN_W // CONV_W
CG_COL = CA_COL + 1
QM_COL = CA_COL + 2

MIB = 1024 * 1024


def _params(semantics, vmem_mib):
    return pltpu.CompilerParams(dimension_semantics=semantics, vmem_limit_bytes=vmem_mib * MIB)


def _dot(a, b):
    return jnp.dot(a, b, preferred_element_type=F32)


def _dot_nt(a, b):
    return lax.dot_general(a, b, (((1,), (1,)), ((), ())), preferred_element_type=F32)


def _split_bf16(a):
    hi = a.astype(BF16)
    lo = (a - hi.astype(F32)).astype(BF16)
    return hi, lo


def _dot_nt_3pass(a, b):
    a_hi, a_lo = _split_bf16(a)
    b_hi, b_lo = _split_bf16(b)
    return _dot_nt(a_hi, b_hi) + (_dot_nt(a_hi, b_lo) + _dot_nt(a_lo, b_hi))


def _sigmoid(x):
    return 1.0 / (1.0 + jnp.exp(-x))


def _rmsnorm(x, g):
    ms = jnp.mean(x * x, axis=-1, keepdims=True)
    return x * lax.rsqrt(ms + NORM_EPS) * g


def _layernorm(x, g, b):
    xc = x - jnp.mean(x, axis=-1, keepdims=True)
    var = jnp.mean(xc * xc, axis=-1, keepdims=True)
    return xc * lax.rsqrt(var + LN_EPS) * g + b


def _norm_matmul_kernel(x_ref, g_ref, w_ref, o_ref, hn_ref):
    @pl.when(pl.program_id(1) == 0)
    def _():
        hn_ref[...] = _rmsnorm(x_ref[...], g_ref[...]).astype(BF16)

    o_ref[...] = _dot(hn_ref[...], w_ref[...])


def _norm_matmul(x, g, w, *, tm, tn):
    m, k = x.shape
    n = w.shape[1]
    return pl.pallas_call(
        _norm_matmul_kernel,
        out_shape=jax.ShapeDtypeStruct((m, n), F32),
        grid=(m // tm, n // tn),
        in_specs=[
            pl.BlockSpec((tm, k), lambda i, j: (i, 0)),
            pl.BlockSpec((1, k), lambda i, j: (0, 0)),
            pl.BlockSpec((k, tn), lambda i, j: (0, j)),
        ],
        out_specs=pl.BlockSpec((tm, tn), lambda i, j: (i, j)),
        scratch_shapes=[pltpu.VMEM((tm, k), BF16)],
        compiler_params=_params(("arbitrary", "arbitrary"), 40),
        name="norm_matmul",
    )(x, g, w)


def _moba_prompt_kernel(q_ref, k_ref, v_ref, o_ref,
                        kb_ref, vt_ref, km_ref, bias_ref, m_ref, l_ref, acc_ref, *, nb):
    i = pl.program_id(2)
    blk = MOBA_BLOCK

    @pl.when(i == 0)
    def _():
        for j in range(nb):
            kj = k_ref[j * blk:(j + 1) * blk, :]
            kb_ref[j] = kj.astype(BF16)
            km_ref[j:j + 1, :] = jnp.mean(kj, axis=0, keepdims=True)
            vt_ref[j] = v_ref[j * blk:(j + 1) * blk, :].T.astype(BF16)

    q = q_ref[...]
    gate = _dot_nt_3pass(km_ref[...], q)
    n_iota = lax.broadcasted_iota(jnp.int32, gate.shape, 0)
    past = n_iota < i
    gm = jnp.where(past, gate, NEG_INF)
    rank = jnp.zeros(gate.shape, jnp.int32)
    for m in range(nb):
        row = gm[m:m + 1, :]
        beats = (row > gm) | ((row == gm) & (m < n_iota))
        rank = rank + beats.astype(jnp.int32)
    sel = past & (rank < MOBA_TOPK)
    bias_ref[...] = jnp.where(sel, 0.0, NEG_INF)

    qb = q.astype(BF16)

    st = _dot_nt(kb_ref[i], qb)
    key_i = lax.broadcasted_iota(jnp.int32, st.shape, 0)
    qry_i = lax.broadcasted_iota(jnp.int32, st.shape, 1)
    st = jnp.where(key_i <= qry_i, st, NEG_INF)
    m0 = jnp.max(st, axis=0, keepdims=True)
    p = jnp.exp2((st - m0) * EXP2_SCALE)
    m_ref[...] = m0
    l_ref[...] = jnp.sum(p, axis=0, keepdims=True)
    acc_ref[...] = _dot(vt_ref[i], p.astype(BF16))

    def body(j, carry):
        sj = _dot_nt(kb_ref[j], qb) + bias_ref[pl.ds(j, 1), :]
        m_prev = m_ref[...]
        m_new = jnp.maximum(m_prev, jnp.max(sj, axis=0, keepdims=True))
        alpha = jnp.exp2((m_prev - m_new) * EXP2_SCALE)
        pj = jnp.exp2((sj - m_new) * EXP2_SCALE)
        l_ref[...] = alpha * l_ref[...] + jnp.sum(pj, axis=0, keepdims=True)
        acc_ref[...] = alpha * acc_ref[...] + _dot(vt_ref[j], pj.astype(BF16))
        m_ref[...] = m_new
        return carry

    lax.fori_loop(0, i, body, 0)
    o_ref[...] = (acc_ref[...] / l_ref[...]).T.astype(o_ref.dtype)


def _moba_prompt(z):
    b, s, _ = z.shape
    nb = s // MOBA_BLOCK
    blk = MOBA_BLOCK
    return pl.pallas_call(
        functools.partial(_moba_prompt_kernel, nb=nb),
        out_shape=jax.ShapeDtypeStruct((b, s, ATTN_W), BF16),
        grid=(b, N_ATTN_HEADS, nb),
        in_specs=[
            pl.BlockSpec((None, blk, HEAD_DIM), lambda bi, h, i: (bi, i, h)),
            pl.BlockSpec((None, s, HEAD_DIM), lambda bi, h, i: (bi, 0, K_COL + h)),
            pl.BlockSpec((None, s, HEAD_DIM), lambda bi, h, i: (bi, 0, V_COL + h)),
        ],
        out_specs=pl.BlockSpec((None, blk, HEAD_DIM), lambda bi, h, i: (bi, i, h)),
        scratch_shapes=[
            pltpu.VMEM((nb, blk, HEAD_DIM), BF16),
            pltpu.VMEM((nb, HEAD_DIM, blk), BF16),
            pltpu.VMEM((nb, HEAD_DIM), F32),
            pltpu.VMEM((nb, blk), F32),
            pltpu.VMEM((1, blk), F32),
            pltpu.VMEM((1, blk), F32),
            pltpu.VMEM((HEAD_DIM, blk), F32),
        ],
        compiler_params=_params(("arbitrary", "arbitrary", "arbitrary"), 32),
        name="moba_prompt",
    )(z, z, z)


def _page_sum_kernel(pt_ref, *refs, pages_per_step):
    del pt_ref
    page_refs = refs[:pages_per_step]
    o_ref = refs[pages_per_step]
    for n in range(pages_per_step // 2):
        s0 = jnp.sum(page_refs[2 * n][...], axis=0, keepdims=True)
        s1 = jnp.sum(page_refs[2 * n + 1][...], axis=0, keepdims=True)
        o_ref[n:n + 1, :] = s0 + s1


def _block_key_sums(cache_k2, page_table):
    _, page, w = cache_k2.shape
    db, n_pages = page_table.shape
    pages_per_block = MOBA_BLOCK // page
    assert pages_per_block == 2
    pages_per_step = 16
    blocks_per_step = pages_per_step // pages_per_block
    steps = n_pages // pages_per_step

    def page_spec(r):
        return pl.BlockSpec((None, page, w),
                            lambda b, g, pt, r=r: (pt[b, g * pages_per_step + r], 0, 0))

    grid_spec = pltpu.PrefetchScalarGridSpec(
        num_scalar_prefetch=1,
        grid=(db, steps),
        in_specs=[page_spec(r) for r in range(pages_per_step)],
        out_specs=pl.BlockSpec((None, blocks_per_step, w), lambda b, g, pt: (b, g, 0)),
    )
    return pl.pallas_call(
        functools.partial(_page_sum_kernel, pages_per_step=pages_per_step),
        out_shape=jax.ShapeDtypeStruct((db, n_pages // pages_per_block, w), F32),
        grid_spec=grid_spec,
        compiler_params=_params(("arbitrary", "arbitrary"), 40),
        name="page_key_sums",
    )(page_table, *([cache_k2] * pages_per_step))


def _moba_select_kernel(q_ref, ks_ref, o_ref):
    km = ks_ref[...] * (1.0 / MOBA_BLOCK)
    prod = km * q_ref[...]
    nblk = prod.shape[0]
    lane = lax.broadcasted_iota(jnp.int32, (nblk, HEAD_DIM), 1)
    gate = jnp.zeros((nblk, HEAD_DIM), F32)
    for h in range(N_ATTN_HEADS):
        gh = jnp.sum(prod[:, h * HEAD_DIM:(h + 1) * HEAD_DIM], axis=-1, keepdims=True)
        gate = jnp.where(lane == h, gh, gate)
    n_iota = lax.broadcasted_iota(jnp.int32, gate.shape, 0)
    rank = jnp.zeros(gate.shape, jnp.int32)
    for m in range(nblk):
        row = gate[m:m + 1, :]
        beats = (row > gate) | ((row == gate) & (m < n_iota))
        rank = rank + beats.astype(jnp.int32)
    out_row = lax.broadcasted_iota(jnp.int32, o_ref.shape, 0)
    out = jnp.zeros(o_ref.shape, jnp.int32)
    for r in range(MOBA_TOPK):
        sel_r = jnp.sum(jnp.where(rank == r, n_iota, 0).astype(F32), axis=0, keepdims=True)
        out = jnp.where(out_row == r, sel_r.astype(jnp.int32), out)
    o_ref[...] = out


def _moba_select(zs, ksum):
    db, nblk, w = ksum.shape
    return pl.pallas_call(
        _moba_select_kernel,
        out_shape=jax.ShapeDtypeStruct((db, 8, HEAD_DIM), jnp.int32),
        grid=(db,),
        in_specs=[
            pl.BlockSpec((None, 1, w), lambda b: (b, 0, 0)),
            pl.BlockSpec((None, nblk, w), lambda b: (b, 0, 0)),
        ],
        out_specs=pl.BlockSpec((None, 8, HEAD_DIM), lambda b: (b, 0, 0)),
        compiler_params=_params(("arbitrary",), 16),
        name="moba_select",
    )(zs, ksum)


def _moba_sample_kernel(pt_ref, sel_ref, q_ref, kn_ref, vn_ref, *refs, n_pages):
    del pt_ref, sel_ref
    k_refs = refs[:n_pages]
    v_refs = refs[n_pages:2 * n_pages]
    o_ref = refs[2 * n_pages]
    q = q_ref[...]
    kk = jnp.concatenate([r[...] for r in k_refs], axis=0).astype(BF16)
    vv = jnp.concatenate([r[...] for r in v_refs], axis=0).astype(BF16)
    q8 = jnp.broadcast_to(q, (8, HEAD_DIM)).astype(BF16)
    s = _dot_nt(q8, kk)[0:1, :] * SCALE
    s_new = jnp.sum(q * kn_ref[...], axis=-1, keepdims=True) * SCALE
    m = jnp.maximum(jnp.max(s, axis=-1, keepdims=True), s_new)
    p = jnp.exp(s - m)
    p_new = jnp.exp(s_new - m)
    l = jnp.sum(p, axis=-1, keepdims=True) + p_new
    p8 = jnp.broadcast_to(p, (8, p.shape[1])).astype(BF16)
    pv = _dot(p8, vv)[0:1, :]
    o_ref[...] = (pv + p_new * vn_ref[...]) / l


def _moba_sample(zs, cache_k2, cache_v2, page_table, sel_flat):
    db = zs.shape[0]
    page = cache_k2.shape[1]
    pages_per_block = MOBA_BLOCK // page
    n_pages = MOBA_TOPK * pages_per_block

    def page_spec(r, t):
        def index_map(b, h, pt, sel):
            blk = sel[(b * MOBA_TOPK + r) * N_ATTN_HEADS + h]
            return (pt[b, blk * pages_per_block + t], 0, h)
        return pl.BlockSpec((None, page, HEAD_DIM), index_map)

    page_specs = [page_spec(r, t) for r in range(MOBA_TOPK) for t in range(pages_per_block)]
    grid_spec = pltpu.PrefetchScalarGridSpec(
        num_scalar_prefetch=2,
        grid=(db, N_ATTN_HEADS),
        in_specs=[
            pl.BlockSpec((None, 1, HEAD_DIM), lambda b, h, pt, sel: (b, 0, h)),
            pl.BlockSpec((None, 1, HEAD_DIM), lambda b, h, pt, sel: (b, 0, K_COL + h)),
            pl.BlockSpec((None, 1, HEAD_DIM), lambda b, h, pt, sel: (b, 0, V_COL + h)),
        ] + page_specs + page_specs,
        out_specs=pl.BlockSpec((None, 1, HEAD_DIM), lambda b, h, pt, sel: (b, 0, h)),
    )
    return pl.pallas_call(
        functools.partial(_moba_sample_kernel, n_pages=n_pages),
        out_shape=jax.ShapeDtypeStruct((db, 1, ATTN_W), F32),
        grid_spec=grid_spec,
        compiler_params=_params(("arbitrary", "arbitrary"), 16),
        name="moba_sample",
    )(page_table, sel_flat, zs, zs, zs, *([cache_k2] * n_pages), *([cache_v2] * n_pages))


CONV_HALO = 32
CONV_CHUNK = 32


def _conv_prompt_kernel(ca_ref, cg_ref, w_ref, b_ref, g_ref, bln_ref, o_ref, tail_ref, ubuf_ref,
                        *, tc):
    i = pl.program_id(1)

    @pl.when(i == 0)
    def _():
        ubuf_ref[0:CONV_HALO, :] = jnp.zeros((CONV_HALO, CONV_W), F32)

    @pl.when(i > 0)
    def _():
        ubuf_ref[0:CONV_HALO, :] = ubuf_ref[tc:tc + CONV_HALO, :]

    ubuf_ref[CONV_HALO:CONV_HALO + tc, :] = ca_ref[...] * _sigmoid(cg_ref[...])
    win = CONV_HALO + CONV_CHUNK
    lead = CONV_HALO - (CONV_K - 1)

    def chunk(c, carry):
        t0 = pl.multiple_of(c * CONV_CHUNK, CONV_CHUNK)
        window = ubuf_ref[pl.ds(t0, win), :]
        acc = jnp.broadcast_to(b_ref[...], (CONV_CHUNK, CONV_W))
        for rm in range(8):
            rolled = window if rm == 0 else pltpu.roll(window, win - rm, axis=0)
            for qd in range(win // 8):
                k = 8 * qd + rm - lead
                if 0 <= k < CONV_K:
                    acc = acc + w_ref[k:k + 1, :] * rolled[8 * qd:8 * qd + CONV_CHUNK, :]
        y = _layernorm(acc, g_ref[...], bln_ref[...])
        o_ref[pl.ds(t0, CONV_CHUNK), :] = (y * _sigmoid(y)).astype(o_ref.dtype)
        return carry

    lax.fori_loop(0, tc // CONV_CHUNK, chunk, 0)
    tail_ref[...] = ubuf_ref[tc:tc + CONV_HALO, :]


def _conv_prompt(z, w, b, g, bln, *, tc):
    bsz, s, _ = z.shape
    vec = pl.BlockSpec((1, CONV_W), lambda bi, i: (0, 0))
    return pl.pallas_call(
        functools.partial(_conv_prompt_kernel, tc=tc),
        out_shape=(jax.ShapeDtypeStruct((bsz, s, CONV_W), BF16),
                   jax.ShapeDtypeStruct((bsz, CONV_HALO, CONV_W), F32)),
        grid=(bsz, s // tc),
        in_specs=[
            pl.BlockSpec((None, tc, CONV_W), lambda bi, i: (bi, i, CA_COL)),
            pl.BlockSpec((None, tc, CONV_W), lambda bi, i: (bi, i, CG_COL)),
            pl.BlockSpec((CONV_K, CONV_W), lambda bi, i: (0, 0)),
            vec, vec, vec,
        ],
        out_specs=(pl.BlockSpec((None, tc, CONV_W), lambda bi, i: (bi, i, 0)),
                   pl.BlockSpec((None, CONV_HALO, CONV_W), lambda bi, i: (bi, 0, 0))),
        scratch_shapes=[pltpu.VMEM((CONV_HALO + tc, CONV_W), F32)],
        compiler_params=_params(("arbitrary", "arbitrary"), 24),
        name="conv_prompt",
    )(z, z, w, b, g, bln)


def _conv_sample_kernel(ca_ref, cg_ref, st_ref, w_ref, b_ref, g_ref, bln_ref, o_ref, u_ref):
    u = ca_ref[...] * _sigmoid(cg_ref[...])
    u_ref[...] = u
    acc = b_ref[...] + w_ref[CONV_K - 1:CONV_K, :] * u
    for k in range(CONV_K - 1):
        acc = acc + w_ref[k:k + 1, :] * st_ref[k]
    y = _layernorm(acc, g_ref[...], bln_ref[...])
    o_ref[...] = y * _sigmoid(y)


def _conv_sample(zs2, state_t, w, b, g, bln):
    db = zs2.shape[0]
    vec = pl.BlockSpec((1, CONV_W), lambda i: (0, 0))
    return pl.pallas_call(
        _conv_sample_kernel,
        out_shape=(jax.ShapeDtypeStruct((db, CONV_W), F32),
                   jax.ShapeDtypeStruct((db, CONV_W), F32)),
        grid=(1,),
        in_specs=[
            pl.BlockSpec((db, CONV_W), lambda i: (0, CA_COL)),
            pl.BlockSpec((db, CONV_W), lambda i: (0, CG_COL)),
            pl.BlockSpec((CONV_K - 1, db, CONV_W), lambda i: (0, 0, 0)),
            pl.BlockSpec((CONV_K, CONV_W), lambda i: (0, 0)),
            vec, vec, vec,
        ],
        out_specs=(pl.BlockSpec((db, CONV_W), lambda i: (0, 0)),
                   pl.BlockSpec((db, CONV_W), lambda i: (0, 0))),
        compiler_params=_params(("arbitrary",), 16),
        name="conv_sample",
    )(zs2, zs2, state_t, w, b, g, bln)


def _mem_attn_kernel(q_ref, mk_ref, mv_ref, o_ref, *, tq):
    rows = max(tq, 8)
    for h in range(N_MEM_HEADS):
        sl = slice(h * HEAD_DIM, (h + 1) * HEAD_DIM)
        q = q_ref[:, sl]
        if rows != tq:
            q = jnp.broadcast_to(q, (rows, HEAD_DIM))
        s = _dot_nt(q.astype(BF16), mk_ref[:, sl].astype(BF16)) * SCALE
        m = jnp.max(s, axis=-1, keepdims=True)
        p = jnp.exp(s - m)
        l = jnp.sum(p, axis=-1, keepdims=True)
        o = _dot(p.astype(BF16), mv_ref[:, sl].astype(BF16)) / l
        o_ref[:, sl] = o[0:tq, :].astype(o_ref.dtype)


def _mem_attn(z, mk, mv, *, tq, k_col, v_col, out_dtype):
    bsz, t, _ = z.shape
    n_mem = mk.shape[1]
    return pl.pallas_call(
        functools.partial(_mem_attn_kernel, tq=tq),
        out_shape=jax.ShapeDtypeStruct((bsz, t, MEM_W), out_dtype),
        grid=(bsz, t // tq),
        in_specs=[
            pl.BlockSpec((None, tq, MEM_W), lambda bi, i: (bi, i, QM_COL)),
            pl.BlockSpec((None, n_mem, MEM_W), lambda bi, i: (bi, 0, k_col)),
            pl.BlockSpec((None, n_mem, MEM_W), lambda bi, i: (bi, 0, v_col)),
        ],
        out_specs=pl.BlockSpec((None, tq, MEM_W), lambda bi, i: (bi, i, 0)),
        compiler_params=_params(("arbitrary", "arbitrary"), 24),
        name="mem_attn",
    )(z, mk, mv)


def _out_proj_kernel(x_ref, oa_ref, oc_ref, om_ref, w_ref, g_ref, x1_ref, h2_ref):
    c0, c1 = ATTN_W, ATTN_W + CONV_W
    mix = _dot(oa_ref[...].astype(BF16), w_ref[0:c0, :])
    mix = mix + _dot(oc_ref[...].astype(BF16), w_ref[c0:c1, :])
    mix = mix + _dot(om_ref[...].astype(BF16), w_ref[c1:, :])
    x1 = x_ref[...] + mix
    x1_ref[...] = x1
    h2_ref[...] = _rmsnorm(x1, g_ref[...]).astype(BF16)


def _out_proj(x, oa, oc, om, w, g, *, tm):
    m, d = x.shape
    row = lambda width: pl.BlockSpec((tm, width), lambda i: (i, 0))
    return pl.pallas_call(
        _out_proj_kernel,
        out_shape=(jax.ShapeDtypeStruct((m, d), F32), jax.ShapeDtypeStruct((m, d), BF16)),
        grid=(m // tm,),
        in_specs=[
            row(d), row(ATTN_W), row(CONV_W), row(MEM_W),
            pl.BlockSpec(w.shape, lambda i: (0, 0)),
            pl.BlockSpec((1, d), lambda i: (0, 0)),
        ],
        out_specs=(row(d), row(d)),
        compiler_params=_params(("arbitrary",), 40),
        name="out_proj",
    )(x, oa, oc, om, w, g)


def _ffn_tail(j, gate, val, wd_ref, x1_ref, gfin_ref, y_ref, acc_ref):
    hmid = (gate * _sigmoid(gate) * val).astype(BF16)
    contrib = _dot(hmid, wd_ref[...])

    @pl.when(j == 0)
    def _():
        acc_ref[...] = contrib

    @pl.when(j > 0)
    def _():
        acc_ref[...] += contrib

    @pl.when(j == pl.num_programs(1) - 1)
    def _():
        y_ref[...] = _rmsnorm(x1_ref[...] + acc_ref[...], gfin_ref[...])


def _ffn_prompt_kernel(h2_ref, x1_ref, wg_ref, wv_ref, wd_ref, wdw_ref, bdw_ref, gfin_ref,
                       y_ref, hist_ref, acc_ref, carry_ref, *, tiles_per_seq):
    i = pl.program_id(0)
    j = pl.program_id(1)
    h2 = h2_ref[...]
    a = _dot(h2, wg_ref[...])
    val = _dot(h2, wv_ref[...])
    tm = a.shape[0]

    @pl.when(i % tiles_per_seq == 0)
    def _():
        carry_ref[j] = jnp.zeros(carry_ref.shape[1:], F32)

    prev = carry_ref[j]
    carry_ref[j] = a[tm - 8:tm, :]
    hist_ref[...] = a[tm - 8:tm, :]
    row = lax.broadcasted_iota(jnp.int32, a.shape, 0)
    a1 = jnp.where(row == 0, prev[7:8, :], pltpu.roll(a, 1, axis=0))
    a2 = jnp.where(row == 0, prev[6:7, :],
                   jnp.where(row == 1, prev[7:8, :], pltpu.roll(a, 2, axis=0)))
    gate = wdw_ref[0:1, :] * a2 + wdw_ref[1:2, :] * a1 + wdw_ref[2:3, :] * a + bdw_ref[...]
    _ffn_tail(j, gate, val, wd_ref, x1_ref, gfin_ref, y_ref, acc_ref)


def _ffn_prompt(h2, x1, wg, wv, wd, wdw, bdw, gfin, *, tm, tf, seq):
    m, d = x1.shape
    dff = wg.shape[1]
    tiles_per_seq = seq // tm
    return pl.pallas_call(
        functools.partial(_ffn_prompt_kernel, tiles_per_seq=tiles_per_seq),
        out_shape=(jax.ShapeDtypeStruct((m, d), F32),
                   jax.ShapeDtypeStruct((m // tm, 8, dff), F32)),
        grid=(m // tm, dff // tf),
        in_specs=[
            pl.BlockSpec((tm, d), lambda i, j: (i, 0)),
            pl.BlockSpec((tm, d), lambda i, j: (i, 0)),
            pl.BlockSpec((d, tf), lambda i, j: (0, j)),
            pl.BlockSpec((d, tf), lambda i, j: (0, j)),
            pl.BlockSpec((tf, d), lambda i, j: (j, 0)),
            pl.BlockSpec((FFN_K, tf), lambda i, j: (0, j)),
            pl.BlockSpec((1, tf), lambda i, j: (0, j)),
            pl.BlockSpec((1, d), lambda i, j: (0, 0)),
        ],
        out_specs=(pl.BlockSpec((tm, d), lambda i, j: (i, 0)),
                   pl.BlockSpec((None, 8, tf), lambda i, j: (i, 0, j))),
        scratch_shapes=[pltpu.VMEM((tm, d), F32), pltpu.VMEM((dff // tf, 8, tf), F32)],
        compiler_params=_params(("arbitrary", "arbitrary"), 56),
        name="ffn_prompt",
    )(h2, x1, wg, wv, wd, wdw, bdw, gfin)


def _ffn_sample_kernel(h2_ref, x1_ref, h0_ref, h1_ref, wg_ref, wv_ref, wd_ref, wdw_ref, bdw_ref,
                       gfin_ref, y_ref, a_ref, acc_ref):
    j = pl.program_id(1)
    h2 = h2_ref[...]
    a = _dot(h2, wg_ref[...])
    val = _dot(h2, wv_ref[...])
    a_ref[...] = a
    gate = (wdw_ref[0:1, :] * h0_ref[...] + wdw_ref[1:2, :] * h1_ref[...]
            + wdw_ref[2:3, :] * a + bdw_ref[...])
    _ffn_tail(j, gate, val, wd_ref, x1_ref, gfin_ref, y_ref, acc_ref)


def _ffn_sample(h2, x1, h0, h1, wg, wv, wd, wdw, bdw, gfin, *, tf):
    m, d = x1.shape
    dff = wg.shape[1]
    return pl.pallas_call(
        _ffn_sample_kernel,
        out_shape=(jax.ShapeDtypeStruct((m, d), F32), jax.ShapeDtypeStruct((m, dff), F32)),
        grid=(1, dff // tf),
        in_specs=[
            pl.BlockSpec((m, d), lambda i, j: (0, 0)),
            pl.BlockSpec((m, d), lambda i, j: (0, 0)),
            pl.BlockSpec((m, tf), lambda i, j: (0, j)),
            pl.BlockSpec((m, tf), lambda i, j: (0, j)),
            pl.BlockSpec((d, tf), lambda i, j: (0, j)),
            pl.BlockSpec((d, tf), lambda i, j: (0, j)),
            pl.BlockSpec((tf, d), lambda i, j: (j, 0)),
            pl.BlockSpec((FFN_K, tf), lambda i, j: (0, j)),
            pl.BlockSpec((1, tf), lambda i, j: (0, j)),
            pl.BlockSpec((1, d), lambda i, j: (0, 0)),
        ],
        out_specs=(pl.BlockSpec((m, d), lambda i, j: (0, 0)),
                   pl.BlockSpec((m, tf), lambda i, j: (0, j))),
        scratch_shapes=[pltpu.VMEM((m, d), F32)],
        compiler_params=_params(("arbitrary", "arbitrary"), 32),
        name="ffn_sample",
    )(h2, x1, h0, h1, wg, wv, wd, wdw, bdw, gfin)


def kernel(x_prompt, x_sample, cache_k, cache_v, state_conv, state_ffn, cache_mem_k, cache_mem_v,
           page_table, mem_prompt, g_mix, w_in, w_conv_dw, b_conv_dw, g_conv_ln, b_conv_ln, w_out,
           g_mem, w_mem_k, w_mem_v, g_ffn, w_ffn_gate, w_ffn_val, w_ffn_dw, b_ffn_dw, w_ffn_down,
           g_final):
    bp, seq, d = x_prompt.shape
    db, dsq, _ = x_sample.shape
    depth = w_in.shape[0]
    n_mem = mem_prompt.shape[1]
    dff = w_ffn_gate.shape[2]
    n_pool, page = cache_k.shape[1], cache_k.shape[2]
    assert dsq == 1 and seq % MOBA_BLOCK == 0
    assert (page_table.shape[1] * page) % MOBA_BLOCK == 0

    xp = x_prompt.reshape(bp * seq, d)
    xs = x_sample.reshape(db, d)
    memp = mem_prompt.reshape(bp * n_mem, d)
    gfin = g_final.reshape(1, d)
    outs = [[] for _ in range(10)]

    for l in range(depth):
        row = lambda v: v[l].reshape(1, -1)
        w_in_b = w_in[l].astype(BF16)
        w_out_b = w_out[l].astype(BF16)
        w_memkv_b = jnp.concatenate([w_mem_k[l], w_mem_v[l]], axis=1).astype(BF16)
        wg_b = w_ffn_gate[l].astype(BF16)
        wv_b = w_ffn_val[l].astype(BF16)
        wd_b = w_ffn_down[l].astype(BF16)
        conv_w = (w_conv_dw[l], row(b_conv_dw), row(g_conv_ln), row(b_conv_ln))
        ffn_w = (wg_b, wv_b, wd_b, w_ffn_dw[l], row(b_ffn_dw), gfin if l == depth - 1 else None)
        assert l == depth - 1, "final RMSNorm is fused into the last layer's feed-forward"

        mkv = _norm_matmul(memp, row(g_mem), w_memkv_b, tm=256, tn=512)
        mkv3 = mkv.reshape(bp, n_mem, 2 * MEM_W)
        zp = _norm_matmul(xp, row(g_mix), w_in_b, tm=512, tn=512).reshape(bp, seq, -1)
        oa = _moba_prompt(zp)
        oc, conv_tail = _conv_prompt(zp, *conv_w, tc=512)
        om = _mem_attn(zp, mkv3, mkv3, tq=512, k_col=0, v_col=1, out_dtype=BF16)
        x1, h2 = _out_proj(xp, oa.reshape(bp * seq, -1), oc.reshape(bp * seq, -1),
                           om.reshape(bp * seq, -1), w_out_b, row(g_ffn), tm=256)
        ffn_tm = 512
        xp, ffn_tail = _ffn_prompt(h2, x1, *ffn_w, tm=ffn_tm, tf=512, seq=seq)

        outs[0].append(zp[:, :, ATTN_W:2 * ATTN_W].reshape(bp, seq, N_ATTN_HEADS, HEAD_DIM))
        outs[1].append(zp[:, :, 2 * ATTN_W:3 * ATTN_W].reshape(bp, seq, N_ATTN_HEADS, HEAD_DIM))
        outs[4].append(conv_tail[:, CONV_HALO - (CONV_K - 1):, :])
        tiles_per_seq = seq // ffn_tm
        outs[6].append(ffn_tail[tiles_per_seq - 1::tiles_per_seq, 8 - (FFN_K - 1):, :])
        outs[8].append(mkv3[:, :, :MEM_W].reshape(bp, n_mem, N_MEM_HEADS, HEAD_DIM))
        outs[9].append(mkv3[:, :, MEM_W:].reshape(bp, n_mem, N_MEM_HEADS, HEAD_DIM))

        zs2 = _norm_matmul(xs, row(g_mix), w_in_b, tm=db, tn=512)
        zs = zs2.reshape(db, 1, -1)
        ck2 = cache_k[l].reshape(n_pool, page, ATTN_W)
        cv2 = cache_v[l].reshape(n_pool, page, ATTN_W)
        ksum = _block_key_sums(ck2, page_table)
        sel = _moba_select(zs, ksum)
        sel_flat = sel[:, :MOBA_TOPK, :N_ATTN_HEADS].reshape(-1)
        oa_s = _moba_sample(zs, ck2, cv2, page_table, sel_flat).reshape(db, ATTN_W)
        oc_s, u_new = _conv_sample(zs2, jnp.transpose(state_conv[l], (1, 0, 2)), *conv_w)
        om_s = _mem_attn(zs, cache_mem_k[l].reshape(db, n_mem, MEM_W),
                         cache_mem_v[l].reshape(db, n_mem, MEM_W),
                         tq=1, k_col=0, v_col=0, out_dtype=F32).reshape(db, MEM_W)
        x1_s, h2_s = _out_proj(xs, oa_s, oc_s, om_s, w_out_b, row(g_ffn), tm=db)
        xs, a_new = _ffn_sample(h2_s, x1_s, state_ffn[l, :, 0, :], state_ffn[l, :, 1, :],
                                *ffn_w, tf=512)

        outs[2].append(zs2[:, ATTN_W:2 * ATTN_W].reshape(db, 1, N_ATTN_HEADS, HEAD_DIM))
        outs[3].append(zs2[:, 2 * ATTN_W:3 * ATTN_W].reshape(db, 1, N_ATTN_HEADS, HEAD_DIM))
        outs[5].append(jnp.concatenate([state_conv[l][:, 1:, :], u_new[:, None, :]], axis=1))
        outs[7].append(jnp.concatenate([state_ffn[l][:, 1:, :], a_new[:, None, :]], axis=1))

    y_prompt = xp.reshape(bp, seq, d)
    y_sample = xs.reshape(db, 1, d)
    st = [jnp.stack(o) for o in outs]
    return (y_prompt, y_sample, st[0], st[1], st[2], st[3], st[4], st[5], st[6], st[7],
            st[8], st[9])
```

```python
import functools
import math

import jax
import jax.numpy as jnp
from jax import lax
from jax.experimental import pallas as pl
from jax.experimental.pallas import tpu as pltpu

F32 = jnp.float32
BF16 = jnp.bfloat16

HEAD_DIM = 128
N_ATTN_HEADS = 8
ATTN_W = N_ATTN_HEADS * HEAD_DIM
N_MEM_HEADS = 4
MEM_W = N_MEM_HEADS * HEAD_DIM
CONV_W = 512
MOBA_BLOCK = 256
MOBA_TOPK = 3
CONV_K = 31
FFN_K = 3
NORM_EPS = 1e-6
LN_EPS = 1e-5
SCALE = HEAD_DIM ** -0.5
EXP2_SCALE = SCALE * math.log2(math.e)
NEG_INF = float("-inf")

K_COL = ATTN_W // HEAD_DIM
V_COL = 2 * ATTN_W // HEAD_DIM
CA_COL = 3 * ATTN_W // CONV_W
CG_COL = CA_COL + 1
QM_COL = CA_COL + 2

MIB = 1024 * 1024


def _params(semantics, vmem_mib):
    return pltpu.CompilerParams(dimension_semantics=semantics, vmem_limit_bytes=vmem_mib * MIB)


def _dot(a, b):
    return jnp.dot(a, b, preferred_element_type=F32)


def _dot_nt(a, b):
    return lax.dot_general(a, b, (((1,), (1,)), ((), ())), preferred_element_type=F32)


def _split_bf16(a):
    hi = a.astype(BF16)
    lo = (a - hi.astype(F32)).astype(BF16)
    return hi, lo


def _dot_nt_3pass(a, b):
    a_hi, a_lo = _split_bf16(a)
    b_hi, b_lo = _split_bf16(b)
    return _dot_nt(a_hi, b_hi) + (_dot_nt(a_hi, b_lo) + _dot_nt(a_lo, b_hi))


def _sigmoid(x):
    return 1.0 / (1.0 + jnp.exp(-x))


def _rmsnorm(x, g):
    ms = jnp.mean(x * x, axis=-1, keepdims=True)
    return x * lax.rsqrt(ms + NORM_EPS) * g


def _layernorm(x, g, b):
    xc = x - jnp.mean(x, axis=-1, keepdims=True)
    var = jnp.mean(xc * xc, axis=-1, keepdims=True)
    return xc * lax.rsqrt(var + LN_EPS) * g + b


def _norm_matmul_kernel(x_ref, g_ref, w_ref, o_ref, hn_ref):
    @pl.when(pl.program_id(1) == 0)
    def _():
        hn_ref[...] = _rmsnorm(x_ref[...], g_ref[...]).astype(BF16)

    o_ref[...] = _dot(hn_ref[...], w_ref[...])


def _norm_matmul(x, g, w, *, tm, tn):
    m, k = x.shape
    n = w.shape[1]
    return pl.pallas_call(
        _norm_matmul_kernel,
        out_shape=jax.ShapeDtypeStruct((m, n), F32),
        grid=(m // tm, n // tn),
        in_specs=[
            pl.BlockSpec((tm, k), lambda i, j: (i, 0)),
            pl.BlockSpec((1, k), lambda i, j: (0, 0)),
            pl.BlockSpec((k, tn), lambda i, j: (0, j)),
        ],
        out_specs=pl.BlockSpec((tm, tn), lambda i, j: (i, j)),
        scratch_shapes=[pltpu.VMEM((tm, k), BF16)],
        compiler_params=_params(("arbitrary", "arbitrary"), 40),
        name="norm_matmul",
    )(x, g, w)


MOBA_GROUP = 2
MASK_VALUE = -2.0 ** 126
MAX_INIT = -2.0 ** 125


MOBA_HEADS_PER_STEP = 4


def _moba_prompt_kernel(q_ref, k_ref, v_ref, o_ref, kaug_ref, vt_ref, km_ref, *, nb):
    i = pl.program_id(2)
    blk = MOBA_BLOCK
    grp = MOBA_GROUP
    heads = range(MOBA_HEADS_PER_STEP)
    cols = lambda hh: slice(hh * HEAD_DIM, (hh + 1) * HEAD_DIM)

    @pl.when(i == 0)
    def _():
        lane = lax.broadcasted_iota(jnp.int32, (blk, HEAD_DIM), 1)
        for hh in heads:
            for j in range(nb):
                kj = k_ref[j * blk:(j + 1) * blk, cols(hh)]
                kaug_ref[hh, j, :, 0:HEAD_DIM] = kj.astype(BF16)
                kaug_ref[hh, j, :, HEAD_DIM:] = jnp.where(lane == j, 1.0, 0.0).astype(BF16)
                km_ref[hh, j:j + 1, :] = jnp.mean(kj, axis=0, keepdims=True)
                vt_ref[hh, j // grp, :, (j % grp) * blk:(j % grp + 1) * blk] = (
                    v_ref[j * blk:(j + 1) * blk, cols(hh)].T.astype(BF16))

    def query_features(hh):
        q = q_ref[:, cols(hh)]
        gate = _dot_nt_3pass(km_ref[hh], q)
        n_iota = lax.broadcasted_iota(jnp.int32, gate.shape, 0)
        past = n_iota < i
        gm = jnp.where(past, gate, NEG_INF)
        rank = jnp.zeros(gate.shape, jnp.int32)
        for m in range(nb):
            row = gm[m:m + 1, :]
            beats = (row > gm) | ((row == gm) & (m < n_iota))
            rank = rank + beats.astype(jnp.int32)
        keep = (past & (rank < MOBA_TOPK)) | (n_iota == i)
        mask = jnp.where(keep, 0.0, MASK_VALUE)
        mask = jnp.concatenate([mask, jnp.zeros((HEAD_DIM - nb, blk), F32)], axis=0)
        return jnp.concatenate([(q * EXP2_SCALE).astype(BF16), mask.T.astype(BF16)], axis=1)

    q_augs = [query_features(hh) for hh in heads]

    def group_one(hh, jj, carry, causal):
        m_prev, l_prev, acc_prev = carry
        scores = []
        for g in range(grp):
            j = jj * grp + g
            s = _dot_nt(kaug_ref[hh, j], q_augs[hh])
            if causal:
                key_r = lax.broadcasted_iota(jnp.int32, s.shape, 0)
                qry_c = lax.broadcasted_iota(jnp.int32, s.shape, 1)
                s = jnp.where(key_r - qry_c <= (i - j) * blk, s, MASK_VALUE)
            scores.append(s)
        m_new = m_prev
        for s in scores:
            m_new = jnp.maximum(m_new, jnp.max(s, axis=0, keepdims=True))
        alpha = jnp.exp2(m_prev - m_new)
        probs = [jnp.exp2(s - m_new) for s in scores]
        l_new = alpha * l_prev
        for p in probs:
            l_new = l_new + jnp.sum(p, axis=0, keepdims=True)
        pcat = jnp.concatenate([p.astype(BF16) for p in probs], axis=0)
        acc_new = alpha * acc_prev + _dot(vt_ref[hh, jj], pcat)
        return m_new, l_new, acc_new

    def group(jj, carries, causal):
        return tuple(group_one(hh, jj, carries[hh], causal) for hh in heads)

    init = tuple((jnp.full((1, blk), MAX_INIT, F32), jnp.zeros((1, blk), F32),
                  jnp.zeros((HEAD_DIM, blk), F32)) for _ in heads)
    carries = lax.fori_loop(0, i // grp, lambda jj, c: group(jj, c, False), init)
    final = group(i // grp, carries, True)
    for hh in heads:
        _, l_fin, acc_fin = final[hh]
        o_ref[:, cols(hh)] = (acc_fin / l_fin).T.astype(o_ref.dtype)


def _moba_prompt(z):
    b, s, _ = z.shape
    nb = s // MOBA_BLOCK
    blk = MOBA_BLOCK
    hp = MOBA_HEADS_PER_STEP
    hw = hp * HEAD_DIM
    assert nb % MOBA_GROUP == 0 and nb <= HEAD_DIM and N_ATTN_HEADS % hp == 0
    k_off, v_off = K_COL // hp, V_COL // hp
    resident = lambda off: pl.BlockSpec((None, s, hw), lambda bi, h, i: (bi, 0, off + h),
                                        pipeline_mode=pl.Buffered(1))
    return pl.pallas_call(
        functools.partial(_moba_prompt_kernel, nb=nb),
        out_shape=jax.ShapeDtypeStruct((b, s, ATTN_W), BF16),
        grid=(b, N_ATTN_HEADS // hp, nb),
        in_specs=[
            pl.BlockSpec((None, blk, hw), lambda bi, h, i: (bi, i, h)),
            resident(k_off),
            resident(v_off),
        ],
        out_specs=pl.BlockSpec((None, blk, hw), lambda bi, h, i: (bi, i, h)),
        scratch_shapes=[
            pltpu.VMEM((hp, nb, blk, 2 * HEAD_DIM), BF16),
            pltpu.VMEM((hp, nb // MOBA_GROUP, HEAD_DIM, MOBA_GROUP * blk), BF16),
            pltpu.VMEM((hp, nb, HEAD_DIM), F32),
        ],
        compiler_params=_params(("arbitrary", "arbitrary", "arbitrary"), 48),
        name="moba_prompt",
    )(z, z, z)


def _page_sum_kernel(pt_ref, *refs, pages_per_step):
    del pt_ref
    page_refs = refs[:pages_per_step]
    o_ref = refs[pages_per_step]
    for n in range(pages_per_step // 2):
        s = jnp.sum(page_refs[2 * n][...], axis=0) + jnp.sum(page_refs[2 * n + 1][...], axis=0)
        for h in range(N_ATTN_HEADS):
            o_ref[n:n + 1, h * HEAD_DIM:(h + 1) * HEAD_DIM] = s[h:h + 1, :]


def _block_key_sums(cache_k, layer, page_table):
    page = cache_k.shape[2]
    w = ATTN_W
    db, n_pages = page_table.shape
    pages_per_block = MOBA_BLOCK // page
    assert pages_per_block == 2
    pages_per_step = 16
    blocks_per_step = pages_per_step // pages_per_block
    steps = n_pages // pages_per_step

    def page_spec(r):
        return pl.BlockSpec(
            (None, None, page, N_ATTN_HEADS, HEAD_DIM),
            lambda b, g, pt, r=r: (layer, pt[b, g * pages_per_step + r], 0, 0, 0))

    grid_spec = pltpu.PrefetchScalarGridSpec(
        num_scalar_prefetch=1,
        grid=(db, steps),
        in_specs=[page_spec(r) for r in range(pages_per_step)],
        out_specs=pl.BlockSpec((None, blocks_per_step, w), lambda b, g, pt: (b, g, 0)),
    )
    return pl.pallas_call(
        functools.partial(_page_sum_kernel, pages_per_step=pages_per_step),
        out_shape=jax.ShapeDtypeStruct((db, n_pages // pages_per_block, w), F32),
        grid_spec=grid_spec,
        compiler_params=_params(("arbitrary", "arbitrary"), 40),
        name="page_key_sums",
    )(page_table, *([cache_k] * pages_per_step))


def _moba_select_kernel(q_ref, ks_ref, o_ref):
    km = ks_ref[...] * (1.0 / MOBA_BLOCK)
    prod = km * q_ref[...]
    nblk = prod.shape[0]
    lane = lax.broadcasted_iota(jnp.int32, (nblk, HEAD_DIM), 1)
    gate = jnp.zeros((nblk, HEAD_DIM), F32)
    for h in range(N_ATTN_HEADS):
        gh = jnp.sum(prod[:, h * HEAD_DIM:(h + 1) * HEAD_DIM], axis=-1, keepdims=True)
        gate = jnp.where(lane == h, gh, gate)
    n_iota = lax.broadcasted_iota(jnp.int32, gate.shape, 0)
    rank = jnp.zeros(gate.shape, jnp.int32)
    for m in range(nblk):
        row = gate[m:m + 1, :]
        beats = (row > gate) | ((row == gate) & (m < n_iota))
        rank = rank + beats.astype(jnp.int32)
    out_row = lax.broadcasted_iota(jnp.int32, o_ref.shape, 0)
    out = jnp.zeros(o_ref.shape, jnp.int32)
    for r in range(MOBA_TOPK):
        sel_r = jnp.sum(jnp.where(rank == r, n_iota, 0).astype(F32), axis=0, keepdims=True)
        out = jnp.where(out_row == r, sel_r.astype(jnp.int32), out)
    o_ref[...] = out


def _moba_select(zs, ksum):
    db, nblk, w = ksum.shape
    return pl.pallas_call(
        _moba_select_kernel,
        out_shape=jax.ShapeDtypeStruct((db, 8, HEAD_DIM), jnp.int32),
        grid=(db,),
        in_specs=[
            pl.BlockSpec((None, 1, w), lambda b: (b, 0, 0)),
            pl.BlockSpec((None, nblk, w), lambda b: (b, 0, 0)),
        ],
        out_specs=pl.BlockSpec((None, 8, HEAD_DIM), lambda b: (b, 0, 0)),
        compiler_params=_params(("arbitrary",), 16),
        name="moba_select",
    )(zs, ksum)


def _moba_sample_kernel(pt_ref, sel_ref, q_ref, kn_ref, vn_ref, k_hbm, v_hbm, o_ref,
                        kbuf, vbuf, sems, *, layer, pages_per_block):
    b = pl.program_id(0)
    h = pl.program_id(1)
    nh = pl.num_programs(1)
    step = b * nh + h
    n_steps = pl.num_programs(0) * nh
    slot = step % 2
    n_pages = MOBA_TOPK * pages_per_block

    def page_copies(bb, hh, sl):
        copies = []
        for r in range(MOBA_TOPK):
            blk = sel_ref[(bb * MOBA_TOPK + r) * N_ATTN_HEADS + hh]
            for t in range(pages_per_block):
                pg = pt_ref[bb, blk * pages_per_block + t]
                n = r * pages_per_block + t
                copies.append(pltpu.make_async_copy(
                    k_hbm.at[layer, pg, :, hh, :], kbuf.at[sl, n], sems.at[0, sl, n]))
                copies.append(pltpu.make_async_copy(
                    v_hbm.at[layer, pg, :, hh, :], vbuf.at[sl, n], sems.at[1, sl, n]))
        return copies

    @pl.when(step == 0)
    def _():
        for c in page_copies(b, h, slot):
            c.start()

    @pl.when(step + 1 < n_steps)
    def _():
        nxt = step + 1
        for c in page_copies(nxt // nh, nxt % nh, 1 - slot):
            c.start()

    for c in page_copies(b, h, slot):
        c.wait()

    q = q_ref[...]
    page = kbuf.shape[2]
    kk = kbuf[slot].reshape(n_pages * page, HEAD_DIM).astype(BF16)
    vv = vbuf[slot].reshape(n_pages * page, HEAD_DIM).astype(BF16)
    q8 = jnp.broadcast_to(q, (8, HEAD_DIM)).astype(BF16)
    s = _dot_nt(q8, kk)[0:1, :] * SCALE
    s_new = jnp.sum(q * kn_ref[...], axis=-1, keepdims=True) * SCALE
    m = jnp.maximum(jnp.max(s, axis=-1, keepdims=True), s_new)
    p = jnp.exp(s - m)
    p_new = jnp.exp(s_new - m)
    l = jnp.sum(p, axis=-1, keepdims=True) + p_new
    p8 = jnp.broadcast_to(p, (8, p.shape[1])).astype(BF16)
    pv = _dot(p8, vv)[0:1, :]
    o_ref[...] = (pv + p_new * vn_ref[...]) / l


def _moba_sample(zs, cache_k, cache_v, layer, page_table, sel_flat):
    db = zs.shape[0]
    page = cache_k.shape[2]
    pages_per_block = MOBA_BLOCK // page
    n_pages = MOBA_TOPK * pages_per_block

    hbm = pl.BlockSpec(memory_space=pl.ANY)
    grid_spec = pltpu.PrefetchScalarGridSpec(
        num_scalar_prefetch=2,
        grid=(db, N_ATTN_HEADS),
        in_specs=[
            pl.BlockSpec((None, 1, HEAD_DIM), lambda b, h, pt, sel: (b, 0, h)),
            pl.BlockSpec((None, 1, HEAD_DIM), lambda b, h, pt, sel: (b, 0, K_COL + h)),
            pl.BlockSpec((None, 1, HEAD_DIM), lambda b, h, pt, sel: (b, 0, V_COL + h)),
            hbm, hbm,
        ],
        out_specs=pl.BlockSpec((None, 1, HEAD_DIM), lambda b, h, pt, sel: (b, 0, h)),
        scratch_shapes=[
            pltpu.VMEM((2, n_pages, page, HEAD_DIM), F32),
            pltpu.VMEM((2, n_pages, page, HEAD_DIM), F32),
            pltpu.SemaphoreType.DMA((2, 2, n_pages)),
        ],
    )
    return pl.pallas_call(
        functools.partial(_moba_sample_kernel, layer=layer, pages_per_block=pages_per_block),
        out_shape=jax.ShapeDtypeStruct((db, 1, ATTN_W), F32),
        grid_spec=grid_spec,
        compiler_params=_params(("arbitrary", "arbitrary"), 16),
        name="moba_sample",
    )(page_table, sel_flat, zs, zs, zs, cache_k, cache_v)


CONV_HALO = 32
CONV_CHUNK = 32


def _conv_prompt_kernel(ca_ref, cg_ref, w_ref, b_ref, g_ref, bln_ref, o_ref, tail_ref, ubuf_ref,
                        *, tc):
    i = pl.program_id(1)

    @pl.when(i == 0)
    def _():
        ubuf_ref[0:CONV_HALO, :] = jnp.zeros((CONV_HALO, CONV_W), F32)

    @pl.when(i > 0)
    def _():
        ubuf_ref[0:CONV_HALO, :] = ubuf_ref[tc:tc + CONV_HALO, :]

    ubuf_ref[CONV_HALO:CONV_HALO + tc, :] = ca_ref[...] * _sigmoid(cg_ref[...])
    win = CONV_HALO + CONV_CHUNK
    lead = CONV_HALO - (CONV_K - 1)

    def chunk(c, carry):
        t0 = pl.multiple_of(c * CONV_CHUNK, CONV_CHUNK)
        window = ubuf_ref[pl.ds(t0, win), :]
        acc = jnp.broadcast_to(b_ref[...], (CONV_CHUNK, CONV_W))
        for rm in range(8):
            rolled = window if rm == 0 else pltpu.roll(window, win - rm, axis=0)
            for qd in range(win // 8):
                k = 8 * qd + rm - lead
                if 0 <= k < CONV_K:
                    acc = acc + w_ref[k:k + 1, :] * rolled[8 * qd:8 * qd + CONV_CHUNK, :]
        y = _layernorm(acc, g_ref[...], bln_ref[...])
        o_ref[pl.ds(t0, CONV_CHUNK), :] = (y * _sigmoid(y)).astype(o_ref.dtype)
        return carry

    lax.fori_loop(0, tc // CONV_CHUNK, chunk, 0)
    tail_ref[...] = ubuf_ref[tc:tc + CONV_HALO, :]


def _conv_prompt(z, w, b, g, bln, *, tc):
    bsz, s, _ = z.shape
    vec = pl.BlockSpec((1, CONV_W), lambda bi, i: (0, 0))
    return pl.pallas_call(
        functools.partial(_conv_prompt_kernel, tc=tc),
        out_shape=(jax.ShapeDtypeStruct((bsz, s, CONV_W), BF16),
                   jax.ShapeDtypeStruct((bsz, CONV_HALO, CONV_W), F32)),
        grid=(bsz, s // tc),
        in_specs=[
            pl.BlockSpec((None, tc, CONV_W), lambda bi, i: (bi, i, CA_COL)),
            pl.BlockSpec((None, tc, CONV_W), lambda bi, i: (bi, i, CG_COL)),
            pl.BlockSpec((CONV_K, CONV_W), lambda bi, i: (0, 0)),
            vec, vec, vec,
        ],
        out_specs=(pl.BlockSpec((None, tc, CONV_W), lambda bi, i: (bi, i, 0)),
                   pl.BlockSpec((None, CONV_HALO, CONV_W), lambda bi, i: (bi, 0, 0))),
        scratch_shapes=[pltpu.VMEM((CONV_HALO + tc, CONV_W), F32)],
        compiler_params=_params(("arbitrary", "arbitrary"), 24),
        name="conv_prompt",
    )(z, z, w, b, g, bln)


def _conv_sample_kernel(ca_ref, cg_ref, st_ref, w_ref, b_ref, g_ref, bln_ref, o_ref, u_ref):
    u = ca_ref[...] * _sigmoid(cg_ref[...])
    u_ref[...] = u
    acc = b_ref[...] + w_ref[CONV_K - 1:CONV_K, :] * u
    for k in range(CONV_K - 1):
        acc = acc + w_ref[k:k + 1, :] * st_ref[k]
    y = _layernorm(acc, g_ref[...], bln_ref[...])
    o_ref[...] = y * _sigmoid(y)


def _conv_sample(zs2, state_t, w, b, g, bln):
    db = zs2.shape[0]
    vec = pl.BlockSpec((1, CONV_W), lambda i: (0, 0))
    return pl.pallas_call(
        _conv_sample_kernel,
        out_shape=(jax.ShapeDtypeStruct((db, CONV_W), F32),
                   jax.ShapeDtypeStruct((db, CONV_W), F32)),
        grid=(1,),
        in_specs=[
            pl.BlockSpec((db, CONV_W), lambda i: (0, CA_COL)),
            pl.BlockSpec((db, CONV_W), lambda i: (0, CG_COL)),
            pl.BlockSpec((CONV_K - 1, db, CONV_W), lambda i: (0, 0, 0)),
            pl.BlockSpec((CONV_K, CONV_W), lambda i: (0, 0)),
            vec, vec, vec,
        ],
        out_specs=(pl.BlockSpec((db, CONV_W), lambda i: (0, 0)),
                   pl.BlockSpec((db, CONV_W), lambda i: (0, 0))),
        compiler_params=_params(("arbitrary",), 16),
        name="conv_sample",
    )(zs2, zs2, state_t, w, b, g, bln)


def _mem_attn_kernel(q_ref, mk_ref, mv_ref, o_ref, *, tq):
    rows = max(tq, 8)
    for h in range(N_MEM_HEADS):
        sl = slice(h * HEAD_DIM, (h + 1) * HEAD_DIM)
        q = q_ref[:, sl]
        if rows != tq:
            q = jnp.broadcast_to(q, (rows, HEAD_DIM))
        s = _dot_nt(q.astype(BF16), mk_ref[:, sl].astype(BF16)) * SCALE
        m = jnp.max(s, axis=-1, keepdims=True)
        p = jnp.exp(s - m)
        l = jnp.sum(p, axis=-1, keepdims=True)
        o = _dot(p.astype(BF16), mv_ref[:, sl].astype(BF16)) / l
        o_ref[:, sl] = o[0:tq, :].astype(o_ref.dtype)


def _mem_attn(z, mk, mv, *, tq, k_col, v_col, out_dtype):
    bsz, t, _ = z.shape
    n_mem = mk.shape[1]
    return pl.pallas_call(
        functools.partial(_mem_attn_kernel, tq=tq),
        out_shape=jax.ShapeDtypeStruct((bsz, t, MEM_W), out_dtype),
        grid=(bsz, t // tq),
        in_specs=[
            pl.BlockSpec((None, tq, MEM_W), lambda bi, i: (bi, i, QM_COL)),
            pl.BlockSpec((None, n_mem, MEM_W), lambda bi, i: (bi, 0, k_col)),
            pl.BlockSpec((None, n_mem, MEM_W), lambda bi, i: (bi, 0, v_col)),
        ],
        out_specs=pl.BlockSpec((None, tq, MEM_W), lambda bi, i: (bi, i, 0)),
        compiler_params=_params(("arbitrary", "arbitrary"), 24),
        name="mem_attn",
    )(z, mk, mv)


def _out_proj_kernel(x_ref, oa_ref, oc_ref, om_ref, w_ref, g_ref, x1_ref, h2_ref):
    c0, c1 = ATTN_W, ATTN_W + CONV_W
    mix = _dot(oa_ref[...].astype(BF16), w_ref[0:c0, :])
    mix = mix + _dot(oc_ref[...].astype(BF16), w_ref[c0:c1, :])
    mix = mix + _dot(om_ref[...].astype(BF16), w_ref[c1:, :])
    x1 = x_ref[...] + mix
    x1_ref[...] = x1
    h2_ref[...] = _rmsnorm(x1, g_ref[...]).astype(BF16)


def _out_proj(x, oa, oc, om, w, g, *, tm):
    m, d = x.shape
    row = lambda width: pl.BlockSpec((tm, width), lambda i: (i, 0))
    return pl.pallas_call(
        _out_proj_kernel,
        out_shape=(jax.ShapeDtypeStruct((m, d), F32), jax.ShapeDtypeStruct((m, d), BF16)),
        grid=(m // tm,),
        in_specs=[
            row(d), row(ATTN_W), row(CONV_W), row(MEM_W),
            pl.BlockSpec(w.shape, lambda i: (0, 0)),
            pl.BlockSpec((1, d), lambda i: (0, 0)),
        ],
        out_specs=(row(d), row(d)),
        compiler_params=_params(("arbitrary",), 40),
        name="out_proj",
    )(x, oa, oc, om, w, g)


def _ffn_tail(j, gate, val, wd_ref, x1_ref, gfin_ref, y_ref, acc_ref):
    hmid = (gate * _sigmoid(gate) * val).astype(BF16)
    contrib = _dot(hmid, wd_ref[...])

    @pl.when(j == 0)
    def _():
        acc_ref[...] = contrib

    @pl.when(j > 0)
    def _():
        acc_ref[...] += contrib

    @pl.when(j == pl.num_programs(1) - 1)
    def _():
        y_ref[...] = _rmsnorm(x1_ref[...] + acc_ref[...], gfin_ref[...])


def _ffn_prompt_kernel(h2_ref, x1_ref, wg_ref, wv_ref, wd_ref, wdw_ref, bdw_ref, gfin_ref,
                       y_ref, hist_ref, acc_ref, carry_ref, *, tiles_per_seq):
    i = pl.program_id(0)
    j = pl.program_id(1)
    h2 = h2_ref[...]
    a = _dot(h2, wg_ref[...])
    val = _dot(h2, wv_ref[...])
    tm = a.shape[0]

    @pl.when(i % tiles_per_seq == 0)
    def _():
        carry_ref[j] = jnp.zeros(carry_ref.shape[1:], F32)

    prev = carry_ref[j]
    carry_ref[j] = a[tm - 8:tm, :]
    hist_ref[...] = a[tm - 8:tm, :]
    row = lax.broadcasted_iota(jnp.int32, a.shape, 0)
    a1 = jnp.where(row == 0, prev[7:8, :], pltpu.roll(a, 1, axis=0))
    a2 = jnp.where(row == 0, prev[6:7, :],
                   jnp.where(row == 1, prev[7:8, :], pltpu.roll(a, 2, axis=0)))
    gate = wdw_ref[0:1, :] * a2 + wdw_ref[1:2, :] * a1 + wdw_ref[2:3, :] * a + bdw_ref[...]
    _ffn_tail(j, gate, val, wd_ref, x1_ref, gfin_ref, y_ref, acc_ref)


def _ffn_prompt(h2, x1, wg, wv, wd, wdw, bdw, gfin, *, tm, tf, seq):
    m, d = x1.shape
    dff = wg.shape[1]
    tiles_per_seq = seq // tm
    return pl.pallas_call(
        functools.partial(_ffn_prompt_kernel, tiles_per_seq=tiles_per_seq),
        out_shape=(jax.ShapeDtypeStruct((m, d), F32),
                   jax.ShapeDtypeStruct((m // tm, 8, dff), F32)),
        grid=(m // tm, dff // tf),
        in_specs=[
            pl.BlockSpec((tm, d), lambda i, j: (i, 0)),
            pl.BlockSpec((tm, d), lambda i, j: (i, 0)),
            pl.BlockSpec((d, tf), lambda i, j: (0, j)),
            pl.BlockSpec((d, tf), lambda i, j: (0, j)),
            pl.BlockSpec((tf, d), lambda i, j: (j, 0)),
            pl.BlockSpec((FFN_K, tf), lambda i, j: (0, j)),
            pl.BlockSpec((1, tf), lambda i, j: (0, j)),
            pl.BlockSpec((1, d), lambda i, j: (0, 0)),
        ],
        out_specs=(pl.BlockSpec((tm, d), lambda i, j: (i, 0)),
                   pl.BlockSpec((None, 8, tf), lambda i, j: (i, 0, j))),
        scratch_shapes=[pltpu.VMEM((tm, d), F32), pltpu.VMEM((dff // tf, 8, tf), F32)],
        compiler_params=_params(("arbitrary", "arbitrary"), 56),
        name="ffn_prompt",
    )(h2, x1, wg, wv, wd, wdw, bdw, gfin)


def _ffn_sample_kernel(h2_ref, x1_ref, h0_ref, h1_ref, wg_ref, wv_ref, wd_ref, wdw_ref, bdw_ref,
                       gfin_ref, y_ref, a_ref, acc_ref):
    j = pl.program_id(1)
    h2 = h2_ref[...]
    a = _dot(h2, wg_ref[...])
    val = _dot(h2, wv_ref[...])
    a_ref[...] = a
    gate = (wdw_ref[0:1, :] * h0_ref[...] + wdw_ref[1:2, :] * h1_ref[...]
            + wdw_ref[2:3, :] * a + bdw_ref[...])
    _ffn_tail(j, gate, val, wd_ref, x1_ref, gfin_ref, y_ref, acc_ref)


def _ffn_sample(h2, x1, h0, h1, wg, wv, wd, wdw, bdw, gfin, *, tf):
    m, d = x1.shape
    dff = wg.shape[1]
    return pl.pallas_call(
        _ffn_sample_kernel,
        out_shape=(jax.ShapeDtypeStruct((m, d), F32), jax.ShapeDtypeStruct((m, dff), F32)),
        grid=(1, dff // tf),
        in_specs=[
            pl.BlockSpec((m, d), lambda i, j: (0, 0)),
            pl.BlockSpec((m, d), lambda i, j: (0, 0)),
            pl.BlockSpec((m, tf), lambda i, j: (0, j)),
            pl.BlockSpec((m, tf), lambda i, j: (0, j)),
            pl.BlockSpec((d, tf), lambda i, j: (0, j)),
            pl.BlockSpec((d, tf), lambda i, j: (0, j)),
            pl.BlockSpec((tf, d), lambda i, j: (j, 0)),
            pl.BlockSpec((FFN_K, tf), lambda i, j: (0, j)),
            pl.BlockSpec((1, tf), lambda i, j: (0, j)),
            pl.BlockSpec((1, d), lambda i, j: (0, 0)),
        ],
        out_specs=(pl.BlockSpec((m, d), lambda i, j: (0, 0)),
                   pl.BlockSpec((m, tf), lambda i, j: (0, j))),
        scratch_shapes=[pltpu.VMEM((m, d), F32)],
        compiler_params=_params(("arbitrary", "arbitrary"), 32),
        name="ffn_sample",
    )(h2, x1, h0, h1, wg, wv, wd, wdw, bdw, gfin)


def kernel(x_prompt, x_sample, cache_k, cache_v, state_conv, state_ffn, cache_mem_k, cache_mem_v,
           page_table, mem_prompt, g_mix, w_in, w_conv_dw, b_conv_dw, g_conv_ln, b_conv_ln, w_out,
           g_mem, w_mem_k, w_mem_v, g_ffn, w_ffn_gate, w_ffn_val, w_ffn_dw, b_ffn_dw, w_ffn_down,
           g_final):
    bp, seq, d = x_prompt.shape
    db, dsq, _ = x_sample.shape
    depth = w_in.shape[0]
    n_mem = mem_prompt.shape[1]
    dff = w_ffn_gate.shape[2]
    page = cache_k.shape[2]
    assert dsq == 1 and seq % MOBA_BLOCK == 0
    assert (page_table.shape[1] * page) % MOBA_BLOCK == 0

    xp = x_prompt.reshape(bp * seq, d)
    xs = x_sample.reshape(db, d)
    memp = mem_prompt.reshape(bp * n_mem, d)
    gfin = g_final.reshape(1, d)
    outs = [[] for _ in range(10)]

    for l in range(depth):
        row = lambda v: v[l].reshape(1, -1)
        w_in_b = w_in[l].astype(BF16)
        w_out_b = w_out[l].astype(BF16)
        w_memkv_b = jnp.concatenate([w_mem_k[l], w_mem_v[l]], axis=1).astype(BF16)
        wg_b = w_ffn_gate[l].astype(BF16)
        wv_b = w_ffn_val[l].astype(BF16)
        wd_b = w_ffn_down[l].astype(BF16)
        conv_w = (w_conv_dw[l], row(b_conv_dw), row(g_conv_ln), row(b_conv_ln))
        ffn_w = (wg_b, wv_b, wd_b, w_ffn_dw[l], row(b_ffn_dw), gfin if l == depth - 1 else None)
        assert l == depth - 1, "final RMSNorm is fused into the last layer's feed-forward"

        mkv = _norm_matmul(memp, row(g_mem), w_memkv_b, tm=256, tn=512)
        mkv3 = mkv.reshape(bp, n_mem, 2 * MEM_W)
        zp = _norm_matmul(xp, row(g_mix), w_in_b, tm=1024, tn=768).reshape(bp, seq, -1)
        oa = _moba_prompt(zp)
        oc, conv_tail = _conv_prompt(zp, *conv_w, tc=512)
        om = _mem_attn(zp, mkv3, mkv3, tq=512, k_col=0, v_col=1, out_dtype=BF16)
        x1, h2 = _out_proj(xp, oa.reshape(bp * seq, -1), oc.reshape(bp * seq, -1),
                           om.reshape(bp * seq, -1), w_out_b, row(g_ffn), tm=256)
        ffn_tm = 512
        xp, ffn_tail = _ffn_prompt(h2, x1, *ffn_w, tm=ffn_tm, tf=512, seq=seq)

        outs[0].append(zp[:, :, ATTN_W:2 * ATTN_W].reshape(bp, seq, N_ATTN_HEADS, HEAD_DIM))
        outs[1].append(zp[:, :, 2 * ATTN_W:3 * ATTN_W].reshape(bp, seq, N_ATTN_HEADS, HEAD_DIM))
        outs[4].append(conv_tail[:, CONV_HALO - (CONV_K - 1):, :])
        tiles_per_seq = seq // ffn_tm
        outs[6].append(ffn_tail[tiles_per_seq - 1::tiles_per_seq, 8 - (FFN_K - 1):, :])
        outs[8].append(mkv3[:, :, :MEM_W].reshape(bp, n_mem, N_MEM_HEADS, HEAD_DIM))
        outs[9].append(mkv3[:, :, MEM_W:].reshape(bp, n_mem, N_MEM_HEADS, HEAD_DIM))

        zs2 = _norm_matmul(xs, row(g_mix), w_in_b, tm=db, tn=512)
        zs = zs2.reshape(db, 1, -1)
        ksum = _block_key_sums(cache_k, l, page_table)
        sel = _moba_select(zs, ksum)
        sel_flat = sel[:, :MOBA_TOPK, :N_ATTN_HEADS].reshape(-1)
        oa_s = _moba_sample(zs, cache_k, cache_v, l, page_table, sel_flat).reshape(db, ATTN_W)
        oc_s, u_new = _conv_sample(zs2, jnp.transpose(state_conv[l], (1, 0, 2)), *conv_w)
        om_s = _mem_attn(zs, cache_mem_k[l].reshape(db, n_mem, MEM_W),
                         cache_mem_v[l].reshape(db, n_mem, MEM_W),
                         tq=1, k_col=0, v_col=0, out_dtype=F32).reshape(db, MEM_W)
        x1_s, h2_s = _out_proj(xs, oa_s, oc_s, om_s, w_out_b, row(g_ffn), tm=db)
        xs, a_new = _ffn_sample(h2_s, x1_s, state_ffn[l, :, 0, :], state_ffn[l, :, 1, :],
                                *ffn_w, tf=512)

        outs[2].append(zs2[:, ATTN_W:2 * ATTN_W].reshape(db, 1, N_ATTN_HEADS, HEAD_DIM))
        outs[3].append(zs2[:, 2 * ATTN_W:3 * ATTN_W].reshape(db, 1, N_ATTN_HEADS, HEAD_DIM))
        outs[5].append(jnp.concatenate([state_conv[l][:, 1:, :], u_new[:, None, :]], axis=1))
        outs[7].append(jnp.concatenate([state_ffn[l][:, 1:, :], a_new[:, None, :]], axis=1))

    y_prompt = xp.reshape(bp, seq, d)
    y_sample = xs.reshape(db, 1, d)
    st = [jnp.stack(o) for o in outs]
    return (y_prompt, y_sample, st[0], st[1], st[2], st[3], st[4], st[5], st[6], st[7],
            st[8], st[9])
```
